```python
import math, functools
import jax, jax.numpy as jnp
from jax import lax
import numpy as np

D_MODEL = 2048
BATCH = 4
SEQ = 2048
DEPTH = 2
DEC_BATCH = 32
DEC_SEQ = 4
PAST_LEN = 8192
PAGE_SIZE = 128

N_HEADS = 16
N_KV_GROUPS = 4
HEADS_PER_GROUP = N_HEADS // N_KV_GROUPS
HEAD_DIM = 128
CMP_BLOCK = 32
CMP_STRIDE = 16
SEL_BLOCK = 64
N_SEL = 16
WINDOW = 512
WIN_Q_BLOCK = 128
SEL_Q_CHUNK = 32
CONV_DIM = 2048
CONV_W = 3
FF_DIM = -(-8 * D_MODEL // (3 * 256)) * 256
PLE_DIM = 256
Q_WIDTH = N_HEADS * HEAD_DIM
KV_WIDTH = N_KV_GROUPS * HEAD_DIM
N_KV_SETS = 6
IN_WIDTH = Q_WIDTH + N_KV_SETS * KV_WIDTH + 3 * N_HEADS + 3 * CONV_DIM + 2 * D_MODEL
DN_ALPHA = (2 * DEPTH) ** 0.25
DN_BETA = (8 * DEPTH) ** -0.25
LN_EPS = 1e-5
NEG = -1e30
FORCE = 1e6

kernel_name = "nsa_shortconv_gated_hybrid_step"


def layer_norm(x, g, b):
    xf = x.astype(jnp.float32)
    mu = jnp.mean(xf, axis=-1, keepdims=True)
    var = jnp.mean(jnp.square(xf - mu), axis=-1, keepdims=True)
    return ((xf - mu) * lax.rsqrt(var + LN_EPS)).astype(x.dtype) * g + b


def pad_rows(a, mult):
    r = (-a.shape[1]) % mult
    return jnp.pad(a, ((0, 0), (0, r)) + ((0, 0),) * (a.ndim - 2))


def compress(k, w_pos, w_phi):
    B, T = k.shape[:2]
    sub = k.reshape(B, T // CMP_STRIDE, CMP_STRIDE, N_KV_GROUPS, HEAD_DIM)
    lo = jnp.einsum('bmigd,i->bmgd', sub, w_pos[:CMP_STRIDE])
    hi = jnp.einsum('bmigd,i->bmgd', sub, w_pos[CMP_STRIDE:])
    return jnp.einsum('bngd,de->bnge', lo[:, :-1] + hi[:, 1:], w_phi)


def cmp_select_attend(q, qpos, kc, vc, n_tokens, gather_sel):
    B, Tq = q.shape[:2]
    scale = HEAD_DIM ** -0.5
    nc = kc.shape[1]
    s = jnp.einsum('btghd,bngd->btghn', q, kc).astype(jnp.float32) * scale
    cend = jnp.arange(nc, dtype=jnp.int32) * CMP_STRIDE + (CMP_BLOCK - 1)
    cm = (cend[None, :] <= qpos[:, None])[None, :, None, None, :]
    p = jnp.where(cm, jax.nn.softmax(jnp.where(cm, s, NEG), axis=-1), 0.0)
    o_cmp = jnp.einsum('btghn,bngd->btghd', p.astype(vc.dtype), vc)
    R = SEL_BLOCK // CMP_STRIDE
    ns = max(-(-n_tokens // SEL_BLOCK), N_SEL)
    imp = jnp.sum(p, axis=3)
    pp = jnp.pad(imp, ((0, 0), (0, 0), (0, 0), (1, R * ns - nc)))
    imp_s = pp[..., :R * ns].reshape(B, Tq, N_KV_GROUPS, ns, R).sum(-1) + pp[..., R::R]
    sblk = jnp.arange(ns, dtype=jnp.int32)[None, :]
    cur = (qpos // SEL_BLOCK)[:, None]
    forced = (sblk == 0) | (sblk == cur) | (sblk == cur - 1)
    valid = sblk * SEL_BLOCK <= qpos[:, None]
    score = jnp.where(forced[None, :, None], FORCE, jnp.where(valid[None, :, None], imp_s, NEG))
    _, idx = lax.top_k(score, N_SEL)
    ks, vs = gather_sel(idx)
    kpos = idx[..., None] * SEL_BLOCK + jnp.arange(SEL_BLOCK, dtype=jnp.int32)
    smask = (kpos <= qpos[None, :, None, None, None]).reshape(B, Tq, N_KV_GROUPS, 1, N_SEL * SEL_BLOCK)
    ss = jnp.einsum('btghd,btgnkd->btghnk', q, ks).astype(jnp.float32) * scale
    ss = ss.reshape(B, Tq, N_KV_GROUPS, HEADS_PER_GROUP, N_SEL * SEL_BLOCK)
    ps = jax.nn.softmax(jnp.where(smask, ss, NEG), axis=-1)
    o_slc = jnp.einsum('btghk,btgkd->btghd', ps.astype(vs.dtype),
                       vs.reshape(B, Tq, N_KV_GROUPS, N_SEL * SEL_BLOCK, HEAD_DIM))
    return o_cmp, o_slc


def make_prompt_gather(k_slc, v_slc):
    B, T = k_slc.shape[:2]
    nb = T // SEL_BLOCK
    kb = k_slc.reshape(B, nb, SEL_BLOCK, N_KV_GROUPS, HEAD_DIM).transpose(0, 3, 1, 2, 4)
    vb = v_slc.reshape(B, nb, SEL_BLOCK, N_KV_GROUPS, HEAD_DIM).transpose(0, 3, 1, 2, 4)
    bi = jnp.arange(B)[:, None, None, None]
    gi = jnp.arange(N_KV_GROUPS)[None, None, :, None]

    def gather(idx):
        i = jnp.clip(idx, 0, nb - 1)
        return kb[bi, gi, i], vb[bi, gi, i]
    return gather


def make_sample_gather(cache_k, cache_v, layer, page_table, k_new, v_new):
    DB = k_new.shape[0]
    nb_past = PAST_LEN // SEL_BLOCK
    blk_per_page = PAGE_SIZE // SEL_BLOCK
    kn = pad_rows(k_new, SEL_BLOCK)
    vn = pad_rows(v_new, SEL_BLOCK)
    nb_new = kn.shape[1] // SEL_BLOCK
    knb = kn.reshape(DB, nb_new, SEL_BLOCK, N_KV_GROUPS, HEAD_DIM).transpose(0, 3, 1, 2, 4)
    vnb = vn.reshape(DB, nb_new, SEL_BLOCK, N_KV_GROUPS, HEAD_DIM).transpose(0, 3, 1, 2, 4)
    bi = jnp.arange(DB)[:, None, None, None]
    gi = jnp.arange(N_KV_GROUPS)[None, None, :, None]

    def gather(idx):
        i = jnp.clip(idx, 0, nb_past + nb_new - 1)
        ip = jnp.minimum(i, nb_past - 1)
        page = page_table[bi, ip // blk_per_page]
        row = (ip % blk_per_page)[..., None] * SEL_BLOCK + jnp.arange(SEL_BLOCK)
        kp = cache_k[layer, page[..., None], row, gi[..., None], :]
        vp = cache_v[layer, page[..., None], row, gi[..., None], :]
        inew = jnp.clip(i - nb_past, 0, nb_new - 1)
        use_new = (i >= nb_past)[..., None, None]
        return (jnp.where(use_new, knb[bi, gi, inew], kp),
                jnp.where(use_new, vnb[bi, gi, inew], vp))
    return gather


def window_prompt(q, k, v):
    B, T = k.shape[:2]
    nq = T // WIN_Q_BLOCK
    nkb = WINDOW // WIN_Q_BLOCK + 1
    padw = ((0, 0), (WINDOW, 0), (0, 0), (0, 0))
    kp = jnp.pad(k, padw).reshape(B, nq + nkb - 1, WIN_Q_BLOCK, N_KV_GROUPS, HEAD_DIM)
    vp = jnp.pad(v, padw).reshape(B, nq + nkb - 1, WIN_Q_BLOCK, N_KV_GROUPS, HEAD_DIM)
    kband = jnp.stack([kp[:, i:i + nq] for i in range(nkb)], axis=2).reshape(B, nq, nkb * WIN_Q_BLOCK, N_KV_GROUPS, HEAD_DIM)
    vband = jnp.stack([vp[:, i:i + nq] for i in range(nkb)], axis=2).reshape(B, nq, nkb * WIN_Q_BLOCK, N_KV_GROUPS, HEAD_DIM)
    qb = q.reshape(B, nq, WIN_Q_BLOCK, N_KV_GROUPS, HEADS_PER_GROUP, HEAD_DIM)
    s = jnp.einsum('bmtghd,bmkgd->bmtghk', qb, kband).astype(jnp.float32) * HEAD_DIM ** -0.5
    blk = jnp.arange(nq, dtype=jnp.int32)[:, None] * WIN_Q_BLOCK
    qpos = blk + jnp.arange(WIN_Q_BLOCK, dtype=jnp.int32)
    kpos = blk - WINDOW + jnp.arange(nkb * WIN_Q_BLOCK, dtype=jnp.int32)
    m = (kpos[:, None, :] >= 0) & (kpos[:, None, :] <= qpos[:, :, None]) & (kpos[:, None, :] > qpos[:, :, None] - WINDOW)
    m = m[None, :, :, None, None, :]
    p = jax.nn.softmax(jnp.where(m, s, NEG), axis=-1)
    o = jnp.einsum('bmtghk,bmkgd->bmtghd', p.astype(vband.dtype), vband)
    return o.reshape(B, T, N_KV_GROUPS, HEADS_PER_GROUP, HEAD_DIM)


def window_sample(q, qpos, k, v, kpos):
    s = jnp.einsum('btghd,bkgd->btghk', q, k).astype(jnp.float32) * HEAD_DIM ** -0.5
    m = (kpos[None, :] <= qpos[:, None]) & (kpos[None, :] > qpos[:, None] - WINDOW)
    p = jax.nn.softmax(jnp.where(m[None, :, None, None, :], s, NEG), axis=-1)
    return jnp.einsum('btghk,bkgd->btghd', p.astype(v.dtype), v)


def short_conv(u, w, b, prev):
    T = u.shape[1]
    ext = jnp.concatenate([prev, u], axis=1)
    y = sum(ext[:, j:j + T] * w[j] for j in range(CONV_W)) + b
    return y, ext[:, -(CONV_W - 1):]


def nsa_prompt(q, kv, w_pos_k, w_pos_v, w_phi_k, w_phi_v):
    B, T = q.shape[:2]
    k_cmp, v_cmp, k_slc, v_slc, k_win, v_win = (kv[:, :, i] for i in range(N_KV_SETS))
    kc = compress(pad_rows(k_cmp, CMP_STRIDE), w_pos_k, w_phi_k)
    vc = compress(pad_rows(v_cmp, CMP_STRIDE), w_pos_v, w_phi_v)
    gather = make_prompt_gather(k_slc, v_slc)
    nck = T // SEL_Q_CHUNK
    qch = q.reshape(B, nck, SEL_Q_CHUNK, N_KV_GROUPS, HEADS_PER_GROUP, HEAD_DIM).transpose(1, 0, 2, 3, 4, 5)
    pos = jnp.arange(T, dtype=jnp.int32).reshape(nck, SEL_Q_CHUNK)

    def chunk(args):
        qc, pc = args
        return cmp_select_attend(qc, pc, kc, vc, T, gather)
    o_cmp, o_slc = lax.map(chunk, (qch, pos))
    shp = (B, T, N_KV_GROUPS, HEADS_PER_GROUP, HEAD_DIM)
    o_cmp = o_cmp.transpose(1, 0, 2, 3, 4, 5).reshape(shp)
    o_slc = o_slc.transpose(1, 0, 2, 3, 4, 5).reshape(shp)
    o_win = window_prompt(q, k_win, v_win)
    wb = min(WINDOW, T)
    return o_cmp, o_slc, o_win, (k_cmp, v_cmp, k_slc, v_slc, k_win[:, -wb:], v_win[:, -wb:])


def nsa_sample(q, kv, layer, cache_k_cmp, cache_v_cmp, cache_k_slc, cache_v_slc, state_k_win, state_v_win,
               page_table, w_pos_k, w_pos_v, w_phi_k, w_phi_v):
    DB, Tn = q.shape[:2]
    k_cmp, v_cmp, k_slc, v_slc, k_win, v_win = (kv[:, :, i] for i in range(N_KV_SETS))
    qpos = PAST_LEN + jnp.arange(Tn, dtype=jnp.int32)
    n_tok = PAST_LEN + Tn

    def full_rows(cache, new):
        past = cache[layer, page_table].reshape(DB, PAST_LEN, N_KV_GROUPS, HEAD_DIM)
        return pad_rows(jnp.concatenate([past, new], axis=1), CMP_STRIDE)
    kc = compress(full_rows(cache_k_cmp, k_cmp), w_pos_k, w_phi_k)
    vc = compress(full_rows(cache_v_cmp, v_cmp), w_pos_v, w_phi_v)
    gather = make_sample_gather(cache_k_slc, cache_v_slc, layer, page_table, k_slc, v_slc)
    o_cmp, o_slc = cmp_select_attend(q, qpos, kc, vc, n_tok, gather)
    wb = state_k_win.shape[2]
    wk = jnp.concatenate([state_k_win[layer], k_win], axis=1)
    wv = jnp.concatenate([state_v_win[layer], v_win], axis=1)
    kpos = PAST_LEN - wb + jnp.arange(wb + Tn, dtype=jnp.int32)
    o_win = window_sample(q, qpos, wk, wv, kpos)
    return o_cmp, o_slc, o_win, (k_cmp, v_cmp, k_slc, v_slc, wk[:, -wb:], wv[:, -wb:])


def layer_forward(x, pe, lp, nsa_fn, conv_prev):
    B, T, _ = x.shape
    z = x @ lp['w_in']
    o0 = Q_WIDTH
    o1 = o0 + N_KV_SETS * KV_WIDTH
    o2 = o1 + 3 * N_HEADS
    o3 = o2 + CONV_DIM
    o4 = o3 + CONV_DIM
    o5 = o4 + CONV_DIM
    o6 = o5 + D_MODEL
    q = z[..., :o0].reshape(B, T, N_KV_GROUPS, HEADS_PER_GROUP, HEAD_DIM)
    kv = z[..., o0:o1].reshape(B, T, N_KV_SETS, N_KV_GROUPS, HEAD_DIM)
    g = jax.nn.sigmoid(z[..., o1:o2]).reshape(B, T, 3, N_KV_GROUPS, HEADS_PER_GROUP, 1)
    h_c, b_c, c_c = z[..., o2:o3], z[..., o3:o4], z[..., o4:o5]
    g_a = jax.nn.sigmoid(z[..., o5:o6])
    g_b = jax.nn.sigmoid(z[..., o6:])
    o_cmp, o_slc, o_win, nsa_state = nsa_fn(q, kv)
    o_nsa = (g[:, :, 0] * o_cmp + g[:, :, 1] * o_slc + g[:, :, 2] * o_win).reshape(B, T, Q_WIDTH)
    conv_y, conv_state = short_conv(c_c * h_c, lp['conv_w'], lp['conv_b'], conv_prev)
    y_conv = b_c * conv_y
    mix = g_a * (o_nsa @ lp['w_br_nsa']) + g_b * (y_conv @ lp['w_br_conv'])
    x1 = layer_norm(DN_ALPHA * x + mix @ lp['w_out'], lp['ln1_g'], lp['ln1_b'])
    ffn = (jax.nn.silu(x1 @ lp['w_ffn_gate']) * (x1 @ lp['w_ffn_up'])) @ lp['w_ffn_down']
    ple = jax.nn.sigmoid(x1 @ lp['w_ple_gate']) * (pe @ lp['w_ple_proj'])
    x2 = layer_norm(DN_ALPHA * x1 + ffn + ple, lp['ln2_g'], lp['ln2_b'])
    return x2, nsa_state, conv_state


def setup_inputs(seed: int = 0) -> dict:
    key = jax.random.key(seed)
    ks = list(jax.random.split(key, 40))

    def nrm(i, shape, scale):
        return jax.random.normal(ks[i], shape, jnp.float32) * scale
    n_pages = PAST_LEN // PAGE_SIZE
    n_used = DEC_BATCH * n_pages
    n_pool = n_used + n_used // 4
    wb = min(WINDOW, PAST_LEN)
    page_table = jax.random.permutation(ks[0], n_pool)[:n_used].reshape(DEC_BATCH, n_pages).astype(jnp.int32)
    cshape = (DEPTH, n_pool, PAGE_SIZE, N_KV_GROUPS, HEAD_DIM)
    return {
        'x_prompt': nrm(1, (BATCH, SEQ, D_MODEL), 1.0),
        'x_sample': nrm(2, (DEC_BATCH, DEC_SEQ, D_MODEL), 1.0),
        'cache_k_cmp': nrm(3, cshape, 1.0),
        'cache_v_cmp': nrm(4, cshape, 1.0),
        'cache_k_slc': nrm(5, cshape, 1.0),
        'cache_v_slc': nrm(6, cshape, 1.0),
        'state_k_win': nrm(7, (DEPTH, DEC_BATCH, wb, N_KV_GROUPS, HEAD_DIM), 1.0),
        'state_v_win': nrm(8, (DEPTH, DEC_BATCH, wb, N_KV_GROUPS, HEAD_DIM), 1.0),
        'state_conv': nrm(9, (DEPTH, DEC_BATCH, CONV_W - 1, CONV_DIM), 1.0),
        'page_table': page_table,
        'p_prompt': nrm(10, (DEPTH, BATCH, SEQ, PLE_DIM), 1.0),
        'p_sample': nrm(11, (DEPTH, DEC_BATCH, DEC_SEQ, PLE_DIM), 1.0),
        'w_in': nrm(12, (DEPTH, D_MODEL, IN_WIDTH), D_MODEL ** -0.5),
        'w_cmp_pos_k': (1.0 + nrm(13, (DEPTH, CMP_BLOCK), 0.1)) * CMP_BLOCK ** -0.5,
        'w_cmp_pos_v': (1.0 + nrm(14, (DEPTH, CMP_BLOCK), 0.1)) * CMP_BLOCK ** -0.5,
        'w_cmp_phi_k': nrm(15, (DEPTH, HEAD_DIM, HEAD_DIM), HEAD_DIM ** -0.5),
        'w_cmp_phi_v': nrm(16, (DEPTH, HEAD_DIM, HEAD_DIM), HEAD_DIM ** -0.5),
        'conv_w': nrm(17, (DEPTH, CONV_W, CONV_DIM), CONV_W ** -0.5),
        'conv_b': nrm(18, (DEPTH, CONV_DIM), 0.02),
        'w_br_nsa': nrm(19, (DEPTH, Q_WIDTH, D_MODEL), Q_WIDTH ** -0.5),
        'w_br_conv': nrm(20, (DEPTH, CONV_DIM, D_MODEL), CONV_DIM ** -0.5),
        'w_out': nrm(21, (DEPTH, D_MODEL, D_MODEL), DN_BETA * D_MODEL ** -0.5),
        'ln1_g': 1.0 + nrm(22, (DEPTH, D_MODEL), 0.02),
        'ln1_b': nrm(23, (DEPTH, D_MODEL), 0.02),
        'w_ffn_gate': nrm(24, (DEPTH, D_MODEL, FF_DIM), D_MODEL ** -0.5),
        'w_ffn_up': nrm(25, (DEPTH, D_MODEL, FF_DIM), D_MODEL ** -0.5),
        'w_ffn_down': nrm(26, (DEPTH, FF_DIM, D_MODEL), DN_BETA * FF_DIM ** -0.5),
        'w_ple_gate': nrm(27, (DEPTH, D_MODEL, D_MODEL), D_MODEL ** -0.5),
        'w_ple_proj': nrm(28, (DEPTH, PLE_DIM, D_MODEL), DN_BETA * PLE_DIM ** -0.5),
        'ln2_g': 1.0 + nrm(29, (DEPTH, D_MODEL), 0.02),
        'ln2_b': nrm(30, (DEPTH, D_MODEL), 0.02),
    }


def reference(x_prompt, x_sample, cache_k_cmp, cache_v_cmp, cache_k_slc, cache_v_slc, state_k_win, state_v_win,
              state_conv, page_table, p_prompt, p_sample, w_in, w_cmp_pos_k, w_cmp_pos_v, w_cmp_phi_k, w_cmp_phi_v,
              conv_w, conv_b, w_br_nsa, w_br_conv, w_out, ln1_g, ln1_b, w_ffn_gate, w_ffn_up, w_ffn_down,
              w_ple_gate, w_ple_proj, ln2_g, ln2_b):
    x_p, x_s = x_prompt, x_sample
    prompt_states, sample_states = [], []
    for l in range(DEPTH):
        lp = {'w_in': w_in[l], 'conv_w': conv_w[l], 'conv_b': conv_b[l], 'w_br_nsa': w_br_nsa[l],
              'w_br_conv': w_br_conv[l], 'w_out': w_out[l], 'ln1_g': ln1_g[l], 'ln1_b': ln1_b[l],
              'w_ffn_gate': w_ffn_gate[l], 'w_ffn_up': w_ffn_up[l], 'w_ffn_down': w_ffn_down[l],
              'w_ple_gate': w_ple_gate[l], 'w_ple_proj': w_ple_proj[l], 'ln2_g': ln2_g[l], 'ln2_b': ln2_b[l]}
        nsa_p = functools.partial(nsa_prompt, w_pos_k=w_cmp_pos_k[l], w_pos_v=w_cmp_pos_v[l],
                                  w_phi_k=w_cmp_phi_k[l], w_phi_v=w_cmp_phi_v[l])
        nsa_s = functools.partial(nsa_sample, layer=l, cache_k_cmp=cache_k_cmp, cache_v_cmp=cache_v_cmp,
                                  cache_k_slc=cache_k_slc, cache_v_slc=cache_v_slc, state_k_win=state_k_win,
                                  state_v_win=state_v_win, page_table=page_table,
                                  w_pos_k=w_cmp_pos_k[l], w_pos_v=w_cmp_pos_v[l],
                                  w_phi_k=w_cmp_phi_k[l], w_phi_v=w_cmp_phi_v[l])
        conv0 = jnp.zeros((x_p.shape[0], CONV_W - 1, CONV_DIM), x_p.dtype)
        x_p, st_p, cv_p = layer_forward(x_p, p_prompt[l], lp, nsa_p, conv0)
        x_s, st_s, cv_s = layer_forward(x_s, p_sample[l], lp, nsa_s, state_conv[l])
        prompt_states.append(st_p + (cv_p,))
        sample_states.append(st_s + (cv_s,))
    nkc_p, nvc_p, nks_p, nvs_p, nkw_p, nvw_p, ncv_p = [jnp.stack(a) for a in zip(*prompt_states)]
    nkc_s, nvc_s, nks_s, nvs_s, nkw_s, nvw_s, ncv_s = [jnp.stack(a) for a in zip(*sample_states)]
    return (x_p, x_s, nkc_p, nvc_p, nks_p, nvs_p, nkw_p, nvw_p, ncv_p,
            nkc_s, nvc_s, nks_s, nvs_s, nkw_s, nvw_s, ncv_s)
```

```python
import functools

import jax
import jax.numpy as jnp
from jax import lax
from jax.experimental import pallas as pl
from jax.experimental.pallas import tpu as pltpu

N_HEADS = 16
N_KV_GROUPS = 4
HEADS_PER_GROUP = N_HEADS // N_KV_GROUPS
HEAD_DIM = 128
GROUP_WIDTH = HEADS_PER_GROUP * HEAD_DIM
CMP_BLOCK = 32
CMP_STRIDE = 16
SEL_BLOCK = 64
SEL_SHIFT = 6
SEL_PER_CMP = SEL_BLOCK // CMP_STRIDE
N_SEL = 16
WINDOW = 512
CONV_W = 3
PAGE_SIZE = 128
N_KV_SETS = 6
LN_EPS = 1e-5
NEG = -1e30
FORCE = 1e6
ATTN_SCALE = HEAD_DIM ** -0.5

LANES = 128
SUBLANES = 8
VMEM_LIMIT_BYTES = 48 * 1024 * 1024
PAGES_PER_STEP = 8

F32 = jnp.float32
BF16 = jnp.bfloat16


def _params(*sem):
    return pltpu.CompilerParams(dimension_semantics=sem, vmem_limit_bytes=VMEM_LIMIT_BYTES)


def _dot(a, b):
    return jnp.dot(a, b, preferred_element_type=F32)


def _dot_nt(a, b):
    return lax.dot_general(a, b, (((1,), (1,)), ((), ())), preferred_element_type=F32)


def _sigmoid(x):
    return 1.0 / (1.0 + jnp.exp(-x))


def _split2(a):
    hi = a.astype(BF16)
    lo = (a - hi.astype(F32)).astype(BF16)
    return hi, lo


def _dot_f32x3(a, b):
    ah, al = _split2(a)
    bh, bl = _split2(b)
    return _dot(ah, bh) + (_dot(ah, bl) + _dot(al, bh))


def _dot_exact01(a, onehot_bf16):
    a1 = a.astype(BF16)
    r1 = a - a1.astype(F32)
    a2 = r1.astype(BF16)
    a3 = (r1 - a2.astype(F32)).astype(BF16)
    return _dot(a1, onehot_bf16) + (_dot(a2, onehot_bf16) + _dot(a3, onehot_bf16))


def _mm_kernel(x_ref, w_ref, *o_refs):
    acc = _dot(x_ref[...], w_ref[...])
    for o_ref in o_refs:
        o_ref[...] = acc.astype(o_ref.dtype)


def _matmul(x, w, out_dtypes, tm, tn, name):
    m, k = x.shape
    n = w.shape[1]
    assert m % tm == 0 and n % tn == 0, (m, n, tm, tn)
    outs = pl.pallas_call(
        _mm_kernel,
        grid=(m // tm, n // tn),
        in_specs=[pl.BlockSpec((tm, k), lambda i, j: (i, 0)),
                  pl.BlockSpec((k, tn), lambda i, j: (0, j))],
        out_specs=[pl.BlockSpec((tm, tn), lambda i, j: (i, j)) for _ in out_dtypes],
        out_shape=[jax.ShapeDtypeStruct((m, n), dt) for dt in out_dtypes],
        compiler_params=_params("parallel", "parallel"),
        name=name,
    )(x, w)
    return outs


def _mix_kernel(a_ref, b_ref, wa_ref, wb_ref, za_ref, zb_ref, o_ref):
    ya = _dot(a_ref[...], wa_ref[...])
    yb = _dot(b_ref[...], wb_ref[...])
    o_ref[...] = (_sigmoid(za_ref[...]) * ya + _sigmoid(zb_ref[...]) * yb).astype(o_ref.dtype)


def _mix(o_nsa, y_conv, w_nsa, w_conv, z_gates, tm, tn):
    m, k = o_nsa.shape
    n = w_nsa.shape[1]
    nb = n // tn
    return pl.pallas_call(
        _mix_kernel,
        grid=(m // tm, nb),
        in_specs=[pl.BlockSpec((tm, k), lambda i, j: (i, 0)),
                  pl.BlockSpec((tm, k), lambda i, j: (i, 0)),
                  pl.BlockSpec((k, tn), lambda i, j: (0, j)),
                  pl.BlockSpec((k, tn), lambda i, j: (0, j)),
                  pl.BlockSpec((tm, tn), lambda i, j: (i, j)),
                  pl.BlockSpec((tm, tn), lambda i, j: (i, j + nb))],
        out_specs=pl.BlockSpec((tm, tn), lambda i, j: (i, j)),
        out_shape=jax.ShapeDtypeStruct((m, n), BF16),
        compiler_params=_params("parallel", "parallel"),
        name="branch_mix",
    )(o_nsa, y_conv, w_nsa, w_conv, z_gates, z_gates)


def _layer_norm_rows(y, g, b):
    mu = jnp.mean(y, axis=-1, keepdims=True)
    d = y - mu
    var = jnp.mean(d * d, axis=-1, keepdims=True)
    return d * lax.rsqrt(var + LN_EPS) * g + b


def _ln1_kernel(x_ref, m_ref, w_ref, g_ref, b_ref, o_ref, ob_ref, *, alpha):
    y = alpha * x_ref[...] + _dot(m_ref[...], w_ref[...])
    out = _layer_norm_rows(y, g_ref[...], b_ref[...])
    o_ref[...] = out
    ob_ref[...] = out.astype(BF16)


def _out_ln1(x, mix, w_out, g, b, alpha, tm):
    m, d = x.shape
    row = lambda i: (i, 0)
    fixed = lambda i: (0, 0)
    return pl.pallas_call(
        functools.partial(_ln1_kernel, alpha=alpha),
        grid=(m // tm,),
        in_specs=[pl.BlockSpec((tm, d), row), pl.BlockSpec((tm, d), row),
                  pl.BlockSpec((d, d), fixed), pl.BlockSpec((1, d), fixed), pl.BlockSpec((1, d), fixed)],
        out_specs=[pl.BlockSpec((tm, d), row), pl.BlockSpec((tm, d), row)],
        out_shape=[jax.ShapeDtypeStruct((m, d), F32), jax.ShapeDtypeStruct((m, d), BF16)],
        compiler_params=_params("parallel"),
        name="out_proj_ln1",
    )(x, mix, w_out, g, b)


def _ffn_up_kernel(x_ref, wg_ref, wu_ref, o_ref):
    x = x_ref[...]
    zg = _dot(x, wg_ref[...])
    zu = _dot(x, wu_ref[...])
    o_ref[...] = (zg * _sigmoid(zg) * zu).astype(o_ref.dtype)


def _ffn_up(x, wg, wu, tm, tn):
    m, k = x.shape
    n = wg.shape[1]
    return pl.pallas_call(
        _ffn_up_kernel,
        grid=(m // tm, n // tn),
        in_specs=[pl.BlockSpec((tm, k), lambda i, j: (i, 0)),
                  pl.BlockSpec((k, tn), lambda i, j: (0, j)),
                  pl.BlockSpec((k, tn), lambda i, j: (0, j))],
        out_specs=pl.BlockSpec((tm, tn), lambda i, j: (i, j)),
        out_shape=jax.ShapeDtypeStruct((m, n), BF16),
        compiler_params=_params("parallel", "parallel"),
        name="ffn_up",
    )(x, wg, wu)


def _ln2_kernel(x_ref, xb_ref, f_ref, pe_ref, wg_ref, wp_ref, g_ref, b_ref, o_ref, ob_ref, *, alpha):
    gate = _sigmoid(_dot(xb_ref[...], wg_ref[...]))
    ple = gate * _dot(pe_ref[...], wp_ref[...])
    y = alpha * x_ref[...] + f_ref[...] + ple
    out = _layer_norm_rows(y, g_ref[...], b_ref[...])
    o_ref[...] = out
    ob_ref[...] = out.astype(BF16)


def _ple_ln2(x1, x1b, ffn, pe, w_gate, w_proj, g, b, alpha, tm):
    m, d = x1.shape
    p = pe.shape[1]
    row = lambda i: (i, 0)
    fixed = lambda i: (0, 0)
    return pl.pallas_call(
        functools.partial(_ln2_kernel, alpha=alpha),
        grid=(m // tm,),
        in_specs=[pl.BlockSpec((tm, d), row), pl.BlockSpec((tm, d), row), pl.BlockSpec((tm, d), row),
                  pl.BlockSpec((tm, p), row), pl.BlockSpec((d, d), fixed), pl.BlockSpec((p, d), fixed),
                  pl.BlockSpec((1, d), fixed), pl.BlockSpec((1, d), fixed)],
        out_specs=[pl.BlockSpec((tm, d), row), pl.BlockSpec((tm, d), row)],
        out_shape=[jax.ShapeDtypeStruct((m, d), F32), jax.ShapeDtypeStruct((m, d), BF16)],
        compiler_params=_params("parallel"),
        name="ple_ln2",
    )(x1, x1b, ffn, pe, w_gate, w_proj, g, b)


def _conv_kernel(h_ref, b_ref, c_ref, prev_ref, w_ref, bias_ref, y_ref, st_ref, ext_ref, *, tr):
    r = pl.program_id(2)

    @pl.when(r == 0)
    def _():
        ext_ref[0:SUBLANES, :] = prev_ref[0]

    @pl.when(r > 0)
    def _():
        ext_ref[0:SUBLANES, :] = ext_ref[tr:tr + SUBLANES, :]

    u = c_ref[0] * h_ref[0]
    ext_ref[SUBLANES:SUBLANES + tr, :] = u
    w = w_ref[...]
    y = (ext_ref[SUBLANES - 2:SUBLANES - 2 + tr, :] * w[0:1, :]
         + ext_ref[SUBLANES - 1:SUBLANES - 1 + tr, :] * w[1:2, :]
         + u * w[2:3, :] + bias_ref[...])
    y_ref[0] = (b_ref[0] * y).astype(y_ref.dtype)
    st_ref[0] = ext_ref[SUBLANES + tr - (CONV_W - 1):SUBLANES + tr, :]


def _short_conv(hbc, prev8, conv_w, conv_b, tr, tc):
    bsz, t, c3 = hbc.shape
    c = c3 // 3
    ncol = c // tc
    return pl.pallas_call(
        functools.partial(_conv_kernel, tr=tr),
        grid=(bsz, ncol, t // tr),
        in_specs=[pl.BlockSpec((1, tr, tc), lambda b, j, r: (b, r, j)),
                  pl.BlockSpec((1, tr, tc), lambda b, j, r: (b, r, j + ncol)),
                  pl.BlockSpec((1, tr, tc), lambda b, j, r: (b, r, j + 2 * ncol)),
                  pl.BlockSpec((1, SUBLANES, tc), lambda b, j, r: (b, 0, j)),
                  pl.BlockSpec((CONV_W, tc), lambda b, j, r: (0, j)),
                  pl.BlockSpec((1, tc), lambda b, j, r: (0, j))],
        out_specs=[pl.BlockSpec((1, tr, tc), lambda b, j, r: (b, r, j)),
                   pl.BlockSpec((1, CONV_W - 1, tc), lambda b, j, r: (b, 0, j))],
        out_shape=[jax.ShapeDtypeStruct((bsz, t, c), BF16),
                   jax.ShapeDtypeStruct((bsz, CONV_W - 1, c), F32)],
        scratch_shapes=[pltpu.VMEM((tr + SUBLANES, tc), F32)],
        compiler_params=_params("parallel", "parallel", "arbitrary"),
        name="short_conv",
    )(hbc, hbc, hbc, prev8, conv_w, conv_b)


def _pool_rows(ref, w_ref, n_blocks, row_stride=1, row_offset=0):
    lo = None
    hi = None
    for i in range(CMP_STRIDE):
        rows = ref[pl.ds(i * row_stride + row_offset, n_blocks, stride=CMP_STRIDE * row_stride), :]
        tl = w_ref[i] * rows
        th = w_ref[CMP_STRIDE + i] * rows
        lo = tl if lo is None else lo + tl
        hi = th if hi is None else hi + th
    return lo, hi


def _pool_first_hi(ref, w_ref, row_stride=1, row_offset=0):
    acc = None
    for i in range(CMP_STRIDE):
        t = w_ref[CMP_STRIDE + i] * ref[pl.ds(i * row_stride + row_offset, 1), :]
        acc = t if acc is None else acc + t
    return acc


def _masked_softmax_rows(s, allow):
    s = jnp.where(allow, s, NEG)
    m = jnp.max(s, axis=-1, keepdims=True)
    e = jnp.where(allow, jnp.exp(s - m), 0.0)
    den = jnp.sum(e, axis=-1, keepdims=True)
    return e, den


def _rank_select(sc, n_blocks, n_keep):
    assert n_blocks & (n_blocks - 1) == 0
    lane = lax.broadcasted_iota(jnp.int32, (1, LANES), 1)
    blk = lane & (n_blocks - 1)
    cnt = jnp.zeros(sc.shape, F32)
    for r in range(1, n_blocks):
        ro = pltpu.roll(sc, r, axis=1)
        tie = jnp.where(blk >= r, 1.0, 0.0)
        cnt = cnt + jnp.where(ro > sc, 1.0, jnp.where(ro == sc, tie, 0.0))
    return jnp.where((cnt < n_keep) & (lane < n_blocks), 1.0, 0.0)


def _importance_matrix(n_cmp, n_lane_blocks):
    assert n_lane_blocks & (n_lane_blocks - 1) == 0
    c_i = lax.broadcasted_iota(jnp.int32, (n_cmp, LANES), 0)
    s_i = lax.broadcasted_iota(jnp.int32, (n_cmp, LANES), 1) & (n_lane_blocks - 1)
    lo = SEL_PER_CMP * s_i - 1
    return jnp.where((c_i >= lo) & (c_i <= lo + SEL_PER_CMP), 1.0, 0.0).astype(BF16)


def _cmp_prompt_kernel(wk_ref, wv_ref, k_ref, v_ref, phik_ref, phiv_ref, kc_ref, vc_ref, sh_ref, *, nb):
    for src, w_ref, phi_ref, dst in ((k_ref, wk_ref, phik_ref, kc_ref), (v_ref, wv_ref, phiv_ref, vc_ref)):
        lo, hi = _pool_rows(src, w_ref, nb)
        sh_ref[0:nb, :] = hi
        sh_ref[nb:nb + SUBLANES, :] = jnp.zeros((SUBLANES, HEAD_DIM), F32)
        pre = lo + sh_ref[pl.ds(1, nb), :]
        dst[0] = _dot_f32x3(pre, phi_ref[...])


KV_COLS = N_KV_GROUPS * HEAD_DIM


def _compress_prompt(kv, w_pos_k, w_pos_v, phi_k, phi_v, bsz, t):
    nb = t // CMP_STRIDE
    smem = pl.BlockSpec(memory_space=pltpu.SMEM)
    return pl.pallas_call(
        functools.partial(_cmp_prompt_kernel, nb=nb),
        grid=(bsz, N_KV_GROUPS),
        in_specs=[smem, smem,
                  pl.BlockSpec((t, HEAD_DIM), lambda b, g: (b, g)),
                  pl.BlockSpec((t, HEAD_DIM), lambda b, g: (b, N_KV_GROUPS + g)),
                  pl.BlockSpec((HEAD_DIM, HEAD_DIM), lambda b, g: (0, 0)),
                  pl.BlockSpec((HEAD_DIM, HEAD_DIM), lambda b, g: (0, 0))],
        out_specs=[pl.BlockSpec((1, nb, HEAD_DIM), lambda b, g: (b, 0, g)),
                   pl.BlockSpec((1, nb, HEAD_DIM), lambda b, g: (b, 0, g))],
        out_shape=[jax.ShapeDtypeStruct((bsz, nb, KV_COLS), F32),
                   jax.ShapeDtypeStruct((bsz, nb, KV_COLS), F32)],
        scratch_shapes=[pltpu.VMEM((nb + SUBLANES, HEAD_DIM), F32)],
        compiler_params=_params("parallel", "parallel"),
        name="compress_prompt",
    )(w_pos_k, w_pos_v, kv, kv, phi_k, phi_v)


def _nsa_prompt_kernel(q_ref, zg_ref, kc_ref, vc_ref, ks_ref, vs_ref, kw_ref, vw_ref, e_ref, o_ref, *, tq, t):
    qi = pl.program_id(2)
    q0 = qi * tq
    qpos = q0 + lax.broadcasted_iota(jnp.int32, (tq, 1), 0)
    n_cmp = t // CMP_STRIDE
    n_sel_blocks = max(-(-t // SEL_BLOCK), N_SEL)

    kc = kc_ref[0].astype(BF16)
    vc = vc_ref[0].astype(BF16)
    cblk = lax.broadcasted_iota(jnp.int32, (1, n_cmp), 1)
    cm = (cblk * CMP_STRIDE + (CMP_BLOCK - 1) <= qpos) & (cblk < n_cmp - 1)

    qs = [q_ref[:, h * HEAD_DIM:(h + 1) * HEAD_DIM] for h in range(HEADS_PER_GROUP)]

    imp = jnp.zeros((tq, n_cmp), F32)
    o_cmp = []
    for h in range(HEADS_PER_GROUP):
        e, den = _masked_softmax_rows(_dot_nt(qs[h], kc) * ATTN_SCALE, cm)
        p = e / jnp.where(den > 0.0, den, 1.0)
        imp = imp + p
        o_cmp.append(_dot(p.astype(BF16), vc))

    imp_s = _dot_exact01(imp, _importance_matrix(n_cmp, n_sel_blocks))
    blk = lax.broadcasted_iota(jnp.int32, (1, LANES), 1) & (n_sel_blocks - 1)
    cur = qpos >> SEL_SHIFT
    forced = (blk == 0) | (blk == cur) | (blk == cur - 1)
    valid = blk * SEL_BLOCK <= qpos
    sc = jnp.where(forced, FORCE, jnp.where(valid, imp_s, NEG))
    sel = _rank_select(sc, n_sel_blocks, N_SEL)
    key_sel = _dot(sel.astype(BF16), e_ref[...])
    kidx = lax.broadcasted_iota(jnp.int32, (1, t), 1)
    allow_s = (key_sel > 0.5) & (kidx <= qpos)

    kw_len = WINDOW + tq
    start = pl.multiple_of(jnp.maximum(q0 - WINDOW, 0), tq)
    kpos = start + lax.broadcasted_iota(jnp.int32, (1, kw_len), 1)
    allow_w = (kpos <= qpos) & (kpos > qpos - WINDOW)
    kw = kw_ref[pl.ds(start, kw_len), :]
    vw = vw_ref[pl.ds(start, kw_len), :]
    ks = ks_ref[...]
    vs = vs_ref[...]

    gate = _sigmoid(zg_ref[...])
    for h in range(HEADS_PER_GROUP):
        e, den = _masked_softmax_rows(_dot_nt(qs[h], ks) * ATTN_SCALE, allow_s)
        o_slc = _dot(e.astype(BF16), vs) / den
        e, den = _masked_softmax_rows(_dot_nt(qs[h], kw) * ATTN_SCALE, allow_w)
        o_win = _dot(e.astype(BF16), vw) / den
        g0 = gate[:, h:h + 1]
        g1 = gate[:, HEADS_PER_GROUP + h:HEADS_PER_GROUP + h + 1]
        g2 = gate[:, 2 * HEADS_PER_GROUP + h:2 * HEADS_PER_GROUP + h + 1]
        o_ref[:, h * HEAD_DIM:(h + 1) * HEAD_DIM] = (g0 * o_cmp[h] + g1 * o_slc + g2 * o_win).astype(o_ref.dtype)


def _nsa_prompt(q, zg, kc, vc, kvb, sel_expand, bsz, t, tq):
    nq = t // tq
    n_sel_blocks = max(-(-t // SEL_BLOCK), N_SEL)
    assert LANES % n_sel_blocks == 0 and t % tq == 0 and WINDOW % tq == 0
    nc = t // CMP_STRIDE
    kvspec = lambda s: pl.BlockSpec((t, HEAD_DIM), lambda b, g, i: (b, s * N_KV_GROUPS + g))
    return pl.pallas_call(
        functools.partial(_nsa_prompt_kernel, tq=tq, t=t),
        grid=(bsz, N_KV_GROUPS, nq),
        in_specs=[pl.BlockSpec((tq, GROUP_WIDTH), lambda b, g, i: (b * nq + i, g)),
                  pl.BlockSpec((tq, LANES), lambda b, g, i: (b * nq + i, g)),
                  pl.BlockSpec((1, nc, HEAD_DIM), lambda b, g, i: (b, 0, g)),
                  pl.BlockSpec((1, nc, HEAD_DIM), lambda b, g, i: (b, 0, g)),
                  kvspec(2), kvspec(3), kvspec(4), kvspec(5),
                  pl.BlockSpec((LANES, t), lambda b, g, i: (0, 0))],
        out_specs=pl.BlockSpec((tq, GROUP_WIDTH), lambda b, g, i: (b * nq + i, g)),
        out_shape=jax.ShapeDtypeStruct(q.shape, BF16),
        compiler_params=_params("parallel", "parallel", "arbitrary"),
        name="nsa_prompt",
    )(q, zg, kc, vc, kvb, kvb, kvb, kvb, sel_expand)


def _cmp_sample_kernel(pt_ref, wk_ref, wv_ref, *refs, n_chunks):
    del pt_ref
    npg = PAGES_PER_STEP
    k_pages = refs[0:npg]
    k_halo, k_new = refs[npg], refs[npg + 1]
    v_pages = refs[npg + 2:2 * npg + 2]
    v_halo, v_new = refs[2 * npg + 2], refs[2 * npg + 3]
    phik_ref, phiv_ref, kc_ref, vc_ref, sh_ref = refs[2 * npg + 4:]
    c = pl.program_id(1)
    sub = PAGE_SIZE // CMP_STRIDE
    nrow = npg * sub
    ng = N_KV_GROUPS
    for pages, halo, new, w_ref, phi_ref, dst in ((k_pages, k_halo, k_new, wk_ref, phik_ref, kc_ref),
                                                  (v_pages, v_halo, v_new, wv_ref, phiv_ref, vc_ref)):
        phi = phi_ref[...]
        for g in range(ng):
            los = []
            for p in range(npg):
                lo, hi = _pool_rows(pages[p], w_ref, sub, ng, g)
                los.append(lo)
                sh_ref[p * sub:(p + 1) * sub, :] = hi
            nxt = jnp.where(c == n_chunks - 1, _pool_first_hi(new.at[0], w_ref, ng, g),
                            _pool_first_hi(halo, w_ref, ng, g))
            sh_ref[nrow:nrow + SUBLANES, :] = jnp.broadcast_to(nxt, (SUBLANES, HEAD_DIM))
            pre = jnp.concatenate(los, axis=0) + sh_ref[pl.ds(1, nrow), :]
            dst[0, :, g * HEAD_DIM:(g + 1) * HEAD_DIM] = _dot_f32x3(pre, phi)


def _compress_sample(page_table, cache_k, cache_v, layer, new16_k, new16_v, w_pos_k, w_pos_v, phi_k, phi_v):
    dbsz, n_pages = page_table.shape
    npg = PAGES_PER_STEP
    assert n_pages % npg == 0
    n_chunks = n_pages // npg
    sub = PAGE_SIZE // CMP_STRIDE
    nrow = npg * sub
    ng = N_KV_GROUPS
    smem = pl.BlockSpec(memory_space=pltpu.SMEM)

    def page_spec(k):
        return pl.BlockSpec((None, None, PAGE_SIZE * ng, HEAD_DIM),
                            lambda b, c, pt: (layer, pt[b, npg * c + k], 0, 0))

    halo_spec = pl.BlockSpec((None, None, CMP_STRIDE * ng, HEAD_DIM),
                             lambda b, c, pt: (layer, pt[b, jnp.minimum(npg * (c + 1), n_pages - 1)], 0, 0))
    new_spec = pl.BlockSpec((1, CMP_STRIDE * ng, HEAD_DIM), lambda b, c, pt: (b, 0, 0))
    phi_spec = pl.BlockSpec((HEAD_DIM, HEAD_DIM), lambda b, c, pt: (0, 0))
    out_spec = pl.BlockSpec((1, nrow, KV_COLS), lambda b, c, pt: (b, c, 0))
    grid_spec = pltpu.PrefetchScalarGridSpec(
        num_scalar_prefetch=1,
        grid=(dbsz, n_chunks),
        in_specs=([smem, smem] + [page_spec(k) for k in range(npg)] + [halo_spec, new_spec]
                  + [page_spec(k) for k in range(npg)] + [halo_spec, new_spec] + [phi_spec, phi_spec]),
        out_specs=[out_spec, out_spec],
        scratch_shapes=[pltpu.VMEM((nrow + SUBLANES, HEAD_DIM), F32)],
    )
    n_cmp = n_pages * sub
    return pl.pallas_call(
        functools.partial(_cmp_sample_kernel, n_chunks=n_chunks),
        grid_spec=grid_spec,
        out_shape=[jax.ShapeDtypeStruct((dbsz, n_cmp, KV_COLS), F32),
                   jax.ShapeDtypeStruct((dbsz, n_cmp, KV_COLS), F32)],
        compiler_params=_params("parallel", "arbitrary"),
        name="compress_sample",
    )(page_table, w_pos_k, w_pos_v, *([cache_k] * npg), cache_k, new16_k,
      *([cache_v] * npg), cache_v, new16_v, phi_k, phi_v)


ROWS_PER_HEAD = SUBLANES
GROUP_ROWS = HEADS_PER_GROUP * ROWS_PER_HEAD


def _pad_keys(x8):
    return jnp.concatenate([x8, jnp.zeros((LANES - SUBLANES, HEAD_DIM), F32)], axis=0).astype(BF16)


def _nsa_sample_a_kernel(q_ref, zg_ref, kc_ref, vc_ref, kst_ref, vst_ref, new_ref, sel_ref, part_ref,
                         *, past_len, n_tok):
    n_cmp = kc_ref.shape[1]
    wb = kst_ref.shape[0]
    tok = lax.broadcasted_iota(jnp.int32, (GROUP_ROWS, 1), 0) & (ROWS_PER_HEAD - 1)
    qpos = past_len + tok
    cblk = lax.broadcasted_iota(jnp.int32, (1, n_cmp), 1)
    cm = cblk * CMP_STRIDE + (CMP_BLOCK - 1) <= qpos
    imp_mat = _importance_matrix(n_cmp, LANES)
    blk = lax.broadcasted_iota(jnp.int32, (1, LANES), 1)
    qpos8 = past_len + lax.broadcasted_iota(jnp.int32, (ROWS_PER_HEAD, 1), 0)
    cur8 = qpos8 >> SEL_SHIFT
    forced = (blk == 0) | (blk == cur8) | (blk == cur8 - 1)
    valid = blk * SEL_BLOCK <= qpos8
    kpos_st = (past_len - wb) + lax.broadcasted_iota(jnp.int32, (1, wb), 1)
    allow_st = (kpos_st <= qpos) & (kpos_st > qpos - WINDOW)
    kpos_nw = past_len + lax.broadcasted_iota(jnp.int32, (1, LANES), 1)
    allow_nw = (kpos_nw <= qpos) & (kpos_nw > qpos - WINDOW) & (kpos_nw < past_len + n_tok)

    scores = []
    for g in range(N_KV_GROUPS):
        cols = slice(g * HEAD_DIM, (g + 1) * HEAD_DIM)
        q = q_ref[0, g]
        gate = _sigmoid(zg_ref[0, g])
        e, den = _masked_softmax_rows(_dot_nt(q, kc_ref[0, :, cols].astype(BF16)) * ATTN_SCALE, cm)
        p = e / jnp.where(den > 0.0, den, 1.0)
        o_cmp = _dot(p.astype(BF16), vc_ref[0, :, cols].astype(BF16))
        imp = p[0:ROWS_PER_HEAD]
        for h in range(1, HEADS_PER_GROUP):
            imp = imp + p[h * ROWS_PER_HEAD:(h + 1) * ROWS_PER_HEAD]
        imp_s = _dot_exact01(imp, imp_mat)
        scores.append(jnp.where(forced, FORCE, jnp.where(valid, imp_s, NEG)))
        k_nw = _pad_keys(new_ref[0, :, 4 * KV_COLS + g * HEAD_DIM:4 * KV_COLS + (g + 1) * HEAD_DIM])
        v_nw = _pad_keys(new_ref[0, :, 5 * KV_COLS + g * HEAD_DIM:5 * KV_COLS + (g + 1) * HEAD_DIM])
        s1 = jnp.where(allow_st, _dot_nt(q, kst_ref[:, cols].astype(BF16)) * ATTN_SCALE, NEG)
        s2 = jnp.where(allow_nw, _dot_nt(q, k_nw) * ATTN_SCALE, NEG)
        m = jnp.maximum(jnp.max(s1, axis=-1, keepdims=True), jnp.max(s2, axis=-1, keepdims=True))
        e1 = jnp.where(allow_st, jnp.exp(s1 - m), 0.0)
        e2 = jnp.where(allow_nw, jnp.exp(s2 - m), 0.0)
        den = jnp.sum(e1, axis=-1, keepdims=True) + jnp.sum(e2, axis=-1, keepdims=True)
        o_win = (_dot(e1.astype(BF16), vst_ref[:, cols].astype(BF16)) + _dot(e2.astype(BF16), v_nw)) / den
        part_ref[0, g] = gate[:, 0:1] * o_cmp + gate[:, 2:3] * o_win

    sc = jnp.concatenate(scores, axis=0)
    sel = _rank_select(sc, LANES, N_SEL - 1)
    for g in range(N_KV_GROUPS):
        sel_ref[0, g] = sel[g * ROWS_PER_HEAD:(g + 1) * ROWS_PER_HEAD]


def _nsa_sample_a(q32, zg32, kc, vc, state_k_win, state_v_win, layer, new8, past_len, n_tok):
    dbsz = q32.shape[0]
    n_cmp = kc.shape[1]
    wb = state_k_win.shape[2]
    assert past_len % SEL_BLOCK == 0 and past_len // SEL_BLOCK == LANES and n_tok <= SUBLANES
    assert max(-(-(past_len + n_tok) // SEL_BLOCK), N_SEL) == LANES + 1
    grp = lambda b: (b, 0, 0, 0)
    return pl.pallas_call(
        functools.partial(_nsa_sample_a_kernel, past_len=past_len, n_tok=n_tok),
        grid=(dbsz,),
        in_specs=[pl.BlockSpec((1, N_KV_GROUPS, GROUP_ROWS, HEAD_DIM), grp),
                  pl.BlockSpec((1, N_KV_GROUPS, GROUP_ROWS, LANES), grp),
                  pl.BlockSpec((1, n_cmp, KV_COLS), lambda b: (b, 0, 0)),
                  pl.BlockSpec((1, n_cmp, KV_COLS), lambda b: (b, 0, 0)),
                  pl.BlockSpec((None, None, wb, KV_COLS), lambda b: (layer, b, 0, 0)),
                  pl.BlockSpec((None, None, wb, KV_COLS), lambda b: (layer, b, 0, 0)),
                  pl.BlockSpec((1, SUBLANES, N_KV_SETS * KV_COLS), lambda b: (b, 0, 0))],
        out_specs=[pl.BlockSpec((1, N_KV_GROUPS, ROWS_PER_HEAD, LANES), grp),
                   pl.BlockSpec((1, N_KV_GROUPS, GROUP_ROWS, HEAD_DIM), grp)],
        out_shape=[jax.ShapeDtypeStruct((dbsz, N_KV_GROUPS, ROWS_PER_HEAD, LANES), F32),
                   jax.ShapeDtypeStruct((dbsz, N_KV_GROUPS, GROUP_ROWS, HEAD_DIM), F32)],
        compiler_params=_params("parallel"),
        name="nsa_sample_cmp_win",
    )(q32, zg32, kc, vc, state_k_win, state_v_win, new8)


def _nsa_sample_b_kernel(pt_ref, q_ref, zg_ref, sel_ref, part_ref, new_ref, e_ref, *refs, n_chunks, past_len, n_tok):
    del pt_ref
    npg = PAGES_PER_STEP
    k_pages = refs[0:npg]
    v_pages = refs[npg:2 * npg]
    o_ref, m_ref, l_ref, acc_ref = refs[2 * npg:]
    c = pl.program_id(1)
    blocks_per_step = npg * PAGE_SIZE // SEL_BLOCK

    @pl.when(c == 0)
    def _():
        m_ref[...] = jnp.full(m_ref.shape, NEG, F32)
        l_ref[...] = jnp.zeros(l_ref.shape, F32)
        acc_ref[...] = jnp.zeros(acc_ref.shape, F32)

    lane = lax.broadcasted_iota(jnp.int32, (1, LANES), 1)
    sel_all = jnp.concatenate([sel_ref[0, g] for g in range(N_KV_GROUPS)], axis=0)
    in_step = (lane >= c * blocks_per_step) & (lane < (c + 1) * blocks_per_step)
    sel_step = jnp.where(in_step, sel_all, 0.0).astype(BF16)
    key_sel = _dot(sel_step, e_ref[...])

    def update(g, s, allow, v):
        m_old = m_ref[g]
        s = jnp.where(allow, s, NEG)
        m_new = jnp.maximum(m_old, jnp.max(s, axis=-1, keepdims=True))
        e = jnp.where(allow, jnp.exp(s - m_new), 0.0)
        a = jnp.exp(m_old - m_new)
        l_ref[g] = a * l_ref[g] + jnp.sum(e, axis=-1, keepdims=True)
        acc_ref[g] = a * acc_ref[g] + _dot(e.astype(BF16), v)
        m_ref[g] = m_new

    for g in range(N_KV_GROUPS):
        cols = slice(g * HEAD_DIM, (g + 1) * HEAD_DIM)
        ks8 = key_sel[g * ROWS_PER_HEAD:(g + 1) * ROWS_PER_HEAD]
        allow = jnp.concatenate([ks8] * HEADS_PER_GROUP, axis=0) > 0.5
        k = jnp.concatenate([k_pages[p][:, cols] for p in range(npg)], axis=0).astype(BF16)
        v = jnp.concatenate([v_pages[p][:, cols] for p in range(npg)], axis=0).astype(BF16)
        update(g, _dot_nt(q_ref[0, g], k) * ATTN_SCALE, allow, v)

    @pl.when(c == n_chunks - 1)
    def _():
        tok = lax.broadcasted_iota(jnp.int32, (GROUP_ROWS, 1), 0) & (ROWS_PER_HEAD - 1)
        kpos = past_len + lane
        allow_new = (kpos <= past_len + tok) & (kpos < past_len + n_tok)
        for g in range(N_KV_GROUPS):
            k_nw = _pad_keys(new_ref[0, :, 2 * KV_COLS + g * HEAD_DIM:2 * KV_COLS + (g + 1) * HEAD_DIM])
            v_nw = _pad_keys(new_ref[0, :, 3 * KV_COLS + g * HEAD_DIM:3 * KV_COLS + (g + 1) * HEAD_DIM])
            update(g, _dot_nt(q_ref[0, g], k_nw) * ATTN_SCALE, allow_new, v_nw)
            gate = _sigmoid(zg_ref[0, g])
            o = part_ref[0, g] + gate[:, 1:2] * (acc_ref[g] / l_ref[g])
            for h in range(HEADS_PER_GROUP):
                c0 = g * GROUP_WIDTH + h * HEAD_DIM
                o_ref[0, :, c0:c0 + HEAD_DIM] = o[h * ROWS_PER_HEAD:h * ROWS_PER_HEAD + n_tok]


def _nsa_sample_b(page_table, q32, zg32, sel, part, new8, sel_expand, cache_k, cache_v, layer, past_len, n_tok):
    dbsz, n_pages = page_table.shape
    npg = PAGES_PER_STEP
    n_chunks = n_pages // npg
    keys_per_step = npg * PAGE_SIZE
    grp = lambda b, c, pt: (b, 0, 0, 0)

    def page_spec(k):
        return pl.BlockSpec((None, None, PAGE_SIZE, KV_COLS), lambda b, c, pt: (layer, pt[b, npg * c + k], 0, 0))

    grid_spec = pltpu.PrefetchScalarGridSpec(
        num_scalar_prefetch=1,
        grid=(dbsz, n_chunks),
        in_specs=([pl.BlockSpec((1, N_KV_GROUPS, GROUP_ROWS, HEAD_DIM), grp),
                   pl.BlockSpec((1, N_KV_GROUPS, GROUP_ROWS, LANES), grp),
                   pl.BlockSpec((1, N_KV_GROUPS, ROWS_PER_HEAD, LANES), grp),
                   pl.BlockSpec((1, N_KV_GROUPS, GROUP_ROWS, HEAD_DIM), grp),
                   pl.BlockSpec((1, SUBLANES, N_KV_SETS * KV_COLS), lambda b, c, pt: (b, 0, 0)),
                   pl.BlockSpec((LANES, keys_per_step), lambda b, c, pt: (0, 0))]
                  + [page_spec(k) for k in range(npg)] + [page_spec(k) for k in range(npg)]),
        out_specs=pl.BlockSpec((1, n_tok, N_HEADS * HEAD_DIM), lambda b, c, pt: (b, 0, 0)),
        scratch_shapes=[pltpu.VMEM((N_KV_GROUPS, GROUP_ROWS, 1), F32),
                        pltpu.VMEM((N_KV_GROUPS, GROUP_ROWS, 1), F32),
                        pltpu.VMEM((N_KV_GROUPS, GROUP_ROWS, HEAD_DIM), F32)],
    )
    return pl.pallas_call(
        functools.partial(_nsa_sample_b_kernel, n_chunks=n_chunks, past_len=past_len, n_tok=n_tok),
        grid_spec=grid_spec,
        out_shape=jax.ShapeDtypeStruct((dbsz, n_tok, N_HEADS * HEAD_DIM), F32),
        compiler_params=_params("parallel", "arbitrary"),
        name="nsa_sample_slc",
    )(page_table, q32, zg32, sel, part, new8, sel_expand, *([cache_k] * npg), *([cache_v] * npg))


def _gate_weight(w_g):
    d = w_g.shape[0]
    w = w_g.reshape(d, 3, N_KV_GROUPS, HEADS_PER_GROUP).transpose(0, 2, 1, 3).reshape(d, N_KV_GROUPS, 3 * HEADS_PER_GROUP)
    w = jnp.pad(w, ((0, 0), (0, 0), (0, LANES - 3 * HEADS_PER_GROUP)))
    return w.reshape(d, N_KV_GROUPS * LANES)


def _layer_weights(l, w_in, w_br_nsa, w_br_conv, w_out, w_ffn_gate, w_ffn_up, w_ffn_down, w_ple_gate, w_ple_proj):
    d = w_in.shape[1]
    cdim = w_br_conv.shape[1]
    o0 = N_HEADS * HEAD_DIM
    o1 = o0 + N_KV_SETS * KV_COLS
    o2 = o1 + 3 * N_HEADS
    o5 = o2 + 3 * cdim
    wi = w_in[l]
    return dict(
        w_q=wi[:, :o0].astype(BF16),
        w_kv=wi[:, o0:o1].astype(BF16),
        w_g=_gate_weight(wi[:, o1:o2]).astype(BF16),
        w_hbc=wi[:, o2:o5].astype(BF16),
        w_gab=wi[:, o5:].astype(BF16),
        w_br_nsa=w_br_nsa[l].astype(BF16), w_br_conv=w_br_conv[l].astype(BF16), w_out=w_out[l].astype(BF16),
        w_ffn_gate=w_ffn_gate[l].astype(BF16), w_ffn_up=w_ffn_up[l].astype(BF16),
        w_ffn_down=w_ffn_down[l].astype(BF16), w_ple_gate=w_ple_gate[l].astype(BF16),
        w_ple_proj=w_ple_proj[l].astype(BF16), d=d)


def _in_proj(xb, lw, tm):
    (q,) = _matmul(xb, lw["w_q"], [BF16], tm, 512, "in_proj_q")
    kv, kvb = _matmul(xb, lw["w_kv"], [F32, BF16], tm, 512, "in_proj_kv")
    (zg,) = _matmul(xb, lw["w_g"], [F32], tm, 512, "in_proj_gates")
    (hbc,) = _matmul(xb, lw["w_hbc"], [F32], tm, 512, "in_proj_conv")
    (zab,) = _matmul(xb, lw["w_gab"], [F32], tm, 512, "in_proj_merge")
    return q, kv, kvb, zg, hbc, zab


def _post_mixers(x, o_nsa, y_conv, zab, pe, lw, ln, alpha, tm, tm_ln):
    mix = _mix(o_nsa, y_conv, lw["w_br_nsa"], lw["w_br_conv"], zab, tm, 512)
    x1, x1b = _out_ln1(x, mix, lw["w_out"], ln["ln1_g"], ln["ln1_b"], alpha, tm_ln)
    hid = _ffn_up(x1b, lw["w_ffn_gate"], lw["w_ffn_up"], tm, 512)
    (ffn,) = _matmul(hid, lw["w_ffn_down"], [F32], min(tm, 512), 512, "ffn_down")
    return _ple_ln2(x1, x1b, ffn, pe, lw["w_ple_gate"], lw["w_ple_proj"], ln["ln2_g"], ln["ln2_b"], alpha, tm_ln)


def kernel(x_prompt, x_sample, cache_k_cmp, cache_v_cmp, cache_k_slc, cache_v_slc, state_k_win, state_v_win, state_conv, page_table, p_prompt, p_sample, w_in, w_cmp_pos_k, w_cmp_pos_v, w_cmp_phi_k, w_cmp_phi_v, conv_w, conv_b, w_br_nsa, w_br_conv, w_out, ln1_g, ln1_b, w_ffn_gate, w_ffn_up, w_ffn_down, w_ple_gate, w_ple_proj, ln2_g, ln2_b):
    depth = w_in.shape[0]
    bsz, t, d = x_prompt.shape
    dbsz, tn, _ = x_sample.shape
    cdim = conv_w.shape[2]
    n_pool = cache_k_cmp.shape[1]
    n_pages = page_table.shape[1]
    past_len = n_pages * PAGE_SIZE
    wb = state_k_win.shape[2]
    alpha = (2 * depth) ** 0.25
    mp, ms = bsz * t, dbsz * tn
    wbp = min(WINDOW, t)

    caches = [cache_k_cmp.reshape(depth, n_pool, PAGE_SIZE * N_KV_GROUPS, HEAD_DIM),
              cache_v_cmp.reshape(depth, n_pool, PAGE_SIZE * N_KV_GROUPS, HEAD_DIM),
              cache_k_slc.reshape(depth, n_pool, PAGE_SIZE, KV_COLS),
              cache_v_slc.reshape(depth, n_pool, PAGE_SIZE, KV_COLS)]
    st_kw = state_k_win.reshape(depth, dbsz, wb, KV_COLS)
    st_vw = state_v_win.reshape(depth, dbsz, wb, KV_COLS)

    n_sel_p = max(-(-t // SEL_BLOCK), N_SEL)
    expand_p = (jnp.arange(LANES)[:, None] == (jnp.arange(t)[None, :] // SEL_BLOCK)).astype(BF16)
    keys_step = PAGES_PER_STEP * PAGE_SIZE
    bps = keys_step // SEL_BLOCK
    expand_s = ((jnp.arange(LANES)[:, None] % bps) == (jnp.arange(keys_step)[None, :] // SEL_BLOCK)).astype(BF16)
    del n_sel_p

    xp = x_prompt.reshape(mp, d)
    xs = x_sample.reshape(ms, d)
    xpb = xp.astype(BF16)
    xsb = xs.astype(BF16)

    p_states, s_states = [], []
    for l in range(depth):
        lw = _layer_weights(l, w_in, w_br_nsa, w_br_conv, w_out, w_ffn_gate, w_ffn_up, w_ffn_down, w_ple_gate, w_ple_proj)
        ln = dict(ln1_g=ln1_g[l][None], ln1_b=ln1_b[l][None], ln2_g=ln2_g[l][None], ln2_b=ln2_b[l][None])
        cw, cb = conv_w[l], conv_b[l][None]
        wpk, wpv, phk, phv = w_cmp_pos_k[l], w_cmp_pos_v[l], w_cmp_phi_k[l], w_cmp_phi_v[l]

        q, kv, kvb, zg, hbc, zab = _in_proj(xpb, lw, 1024)
        prev0 = jnp.zeros((bsz, SUBLANES, cdim), F32)
        y_conv, conv_st_p = _short_conv(hbc.reshape(bsz, t, 3 * cdim), prev0, cw, cb, 256, 512)
        kc, vc = _compress_prompt(kv, wpk, wpv, phk, phv, bsz, t)
        o_nsa = _nsa_prompt(q, zg, kc, vc, kvb, expand_p, bsz, t, 256)
        xp, xpb = _post_mixers(xp, o_nsa, y_conv.reshape(mp, cdim), zab, p_prompt[l].reshape(mp, -1).astype(BF16),
                               lw, ln, alpha, 1024, 256)
        kv5 = kv.reshape(bsz, t, N_KV_SETS, N_KV_GROUPS, HEAD_DIM)
        p_states.append((kv5[:, :, 0], kv5[:, :, 1], kv5[:, :, 2], kv5[:, :, 3],
                         kv5[:, t - wbp:, 4], kv5[:, t - wbp:, 5], conv_st_p))

        q, kv, kvb, zg, hbc, zab = _in_proj(xsb, lw, ms)
        del kvb
        prev_s = jnp.pad(state_conv[l], ((0, 0), (SUBLANES - (CONV_W - 1), 0), (0, 0)))
        y_conv, conv_st_s = _short_conv(hbc.reshape(dbsz, tn, 3 * cdim), prev_s, cw, cb, tn, 512)
        kv3 = kv.reshape(dbsz, tn, N_KV_SETS * KV_COLS)
        new16_k = jnp.pad(kv3[:, :, 0:KV_COLS], ((0, 0), (0, CMP_STRIDE - tn), (0, 0)))
        new16_v = jnp.pad(kv3[:, :, KV_COLS:2 * KV_COLS], ((0, 0), (0, CMP_STRIDE - tn), (0, 0)))
        new16_k = new16_k.reshape(dbsz, CMP_STRIDE * N_KV_GROUPS, HEAD_DIM)
        new16_v = new16_v.reshape(dbsz, CMP_STRIDE * N_KV_GROUPS, HEAD_DIM)
        new8 = jnp.pad(kv3, ((0, 0), (0, SUBLANES - tn), (0, 0)))
        kc, vc = _compress_sample(page_table, caches[0], caches[1], l, new16_k, new16_v, wpk, wpv, phk, phv)
        q32 = q.reshape(dbsz, tn, N_KV_GROUPS, HEADS_PER_GROUP, HEAD_DIM).transpose(0, 2, 3, 1, 4)
        q32 = jnp.pad(q32, ((0, 0), (0, 0), (0, 0), (0, ROWS_PER_HEAD - tn), (0, 0)))
        q32 = q32.reshape(dbsz, N_KV_GROUPS, GROUP_ROWS, HEAD_DIM)
        zg32 = zg.reshape(dbsz, tn, N_KV_GROUPS, LANES)[..., :3 * HEADS_PER_GROUP]
        zg32 = zg32.reshape(dbsz, tn, N_KV_GROUPS, 3, HEADS_PER_GROUP).transpose(0, 2, 4, 1, 3)
        zg32 = jnp.pad(zg32, ((0, 0), (0, 0), (0, 0), (0, ROWS_PER_HEAD - tn), (0, LANES - 3)))
        zg32 = zg32.reshape(dbsz, N_KV_GROUPS, GROUP_ROWS, LANES)
        sel, part = _nsa_sample_a(q32, zg32, kc, vc, st_kw, st_vw, l, new8, past_len, tn)
        o_nsa = _nsa_sample_b(page_table, q32, zg32, sel, part, new8, expand_s, caches[2], caches[3], l, past_len, tn)
        o_nsa = o_nsa.reshape(ms, N_HEADS * HEAD_DIM).astype(BF16)
        xs, xsb = _post_mixers(xs, o_nsa, y_conv.reshape(ms, cdim), zab, p_sample[l].reshape(ms, -1).astype(BF16),
                               lw, ln, alpha, ms, ms)
        kv5 = kv.reshape(dbsz, tn, N_KV_SETS, N_KV_GROUPS, HEAD_DIM)
        wk = jnp.concatenate([state_k_win[l], kv5[:, :, 4]], axis=1)[:, -wb:]
        wv = jnp.concatenate([state_v_win[l], kv5[:, :, 5]], axis=1)[:, -wb:]
        s_states.append((kv5[:, :, 0], kv5[:, :, 1], kv5[:, :, 2], kv5[:, :, 3], wk, wv, conv_st_s))

    outs_p = [jnp.stack(a) for a in zip(*p_states)]
    outs_s = [jnp.stack(a) for a in zip(*s_states)]
    return (xp.reshape(bsz, t, d), xs.reshape(dbsz, tn, d), *outs_p, *outs_s)
```
